```python
import jax, jax.numpy as jnp
from jax import lax
import numpy as np

D_MODEL = 2048
BATCH = 8
SEQ = 4096
DEPTH = 4

MIX_W = D_MODEL
ATT_W = MIX_W // 2
HEAD_DIM = 128
N_ATT_HEADS = ATT_W // HEAD_DIM
CONV_W = MIX_W // 4
CONV_TAPS = 3
POOL_W = MIX_W - ATT_W - CONV_W
POOL_WINDOWS = (2, 4, 8, 16)
POOL_GROUP = POOL_W // len(POOL_WINDOWS)
Q_BLOCK = 128
LN_EPS = 1e-5
DEEPNORM_ALPHA = (2 * DEPTH) ** 0.25
DEEPNORM_BETA = (8 * DEPTH) ** -0.25

IN_WIDTHS = (ATT_W, ATT_W, ATT_W, ATT_W, N_ATT_HEADS,
             CONV_W, CONV_W, CONV_W, CONV_W, POOL_W, POOL_W)
IN_W = sum(IN_WIDTHS)
IN_SPLITS = tuple(sum(IN_WIDTHS[:i + 1]) for i in range(len(IN_WIDTHS) - 1))

kernel_name = "hybrid_fox_shortconv_pool_deepnorm"


def layer_norm(x, g, b):
    x32 = x.astype(jnp.float32)
    mu = jnp.mean(x32, axis=-1, keepdims=True)
    var = jnp.mean(jnp.square(x32 - mu), axis=-1, keepdims=True)
    return ((x32 - mu) * lax.rsqrt(var + LN_EPS) * g + b).astype(x.dtype)


def forgetting_attention(q, k, v, fg_logit, b_f):
    b, s, h, dh = q.shape
    nb = s // Q_BLOCK
    log_f = jax.nn.log_sigmoid(fg_logit.astype(jnp.float32) + b_f.astype(jnp.float32))
    cum = jnp.cumsum(log_f, axis=1)
    cum_k = jnp.transpose(cum, (0, 2, 1))
    q_blocks = q.reshape(b, nb, Q_BLOCK, h, dh).transpose(1, 0, 2, 3, 4)
    c_blocks = cum.reshape(b, nb, Q_BLOCK, h).transpose(1, 0, 3, 2)
    k_pos = jnp.arange(s)
    scale = HEAD_DIM ** -0.5

    def block(args):
        qb, cb, i = args
        logits = jnp.einsum('bqhd,bkhd->bhqk', qb, k).astype(jnp.float32) * scale
        logits = logits + cb[..., None] - cum_k[:, :, None, :]
        q_pos = i * Q_BLOCK + jnp.arange(Q_BLOCK)
        causal = k_pos[None, :] <= q_pos[:, None]
        logits = jnp.where(causal, logits, -jnp.inf)
        p = jax.nn.softmax(logits, axis=-1).astype(v.dtype)
        return jnp.einsum('bhqk,bkhd->bqhd', p, v)

    out = lax.map(block, (q_blocks, c_blocks, jnp.arange(nb)))
    return out.transpose(1, 0, 2, 3, 4).reshape(b, s, h * dh)


def short_conv_mixer(gate_b, gate_c, h, conv_w):
    s = h.shape[1]
    u = gate_c * h
    up = jnp.pad(u, ((0, 0), (CONV_TAPS - 1, 0), (0, 0)))
    y = conv_w[0] * up[:, 0:s]
    for j in range(1, CONV_TAPS):
        y = y + conv_w[j] * up[:, j:j + s]
    return gate_b * y


def multiscale_pool_mixer(u, pool_w, pool_scale):
    s = u.shape[1]
    u32 = u.astype(jnp.float32)
    cs = jnp.pad(jnp.cumsum(u32, axis=1), ((0, 0), (1, 0), (0, 0)))
    t1 = jnp.arange(1, s + 1, dtype=jnp.float32)
    outs = []
    for g, w in enumerate(POOL_WINDOWS):
        sl = slice(g * POOL_GROUP, (g + 1) * POOL_GROUP)
        csg = cs[:, :, sl]
        lagged = jnp.pad(csg[:, :s - w + 1], ((0, 0), (w - 1, 0), (0, 0)))
        count = jnp.minimum(t1, float(w))
        mean = (csg[:, 1:] - lagged) / count[None, :, None]
        z = (mean - u32[:, :, sl]).astype(u.dtype)
        outs.append(jnp.einsum('bsc,cd->bsd', z, pool_w[g]))
    return jnp.concatenate(outs, axis=-1) * pool_scale


def setup_inputs(seed: int = 0) -> dict:
    key = jax.random.key(seed)
    ks = jax.random.split(key, 10)
    x = jax.random.normal(ks[0], (BATCH, SEQ, D_MODEL), jnp.float32)
    w_in = jax.random.normal(ks[1], (DEPTH, D_MODEL, IN_W), jnp.float32) * D_MODEL ** -0.5
    b_f = jax.random.uniform(ks[2], (DEPTH, N_ATT_HEADS), jnp.float32, 1.0, 4.0)
    conv_w = jax.random.normal(ks[3], (DEPTH, CONV_TAPS, CONV_W), jnp.float32) * CONV_TAPS ** -0.5
    pool_w = jax.random.normal(ks[4], (DEPTH, len(POOL_WINDOWS), POOL_GROUP, POOL_GROUP),
                               jnp.float32) * POOL_GROUP ** -0.5
    pool_scale = 1.0 + 0.1 * jax.random.normal(ks[5], (DEPTH, POOL_W), jnp.float32)
    w_out = jax.random.normal(ks[6], (DEPTH, MIX_W, D_MODEL), jnp.float32) * (
        MIX_W ** -0.5 * DEEPNORM_BETA)
    ln_g = 1.0 + 0.1 * jax.random.normal(ks[7], (DEPTH, D_MODEL), jnp.float32)
    ln_b = 0.02 * jax.random.normal(ks[8], (DEPTH, D_MODEL), jnp.float32)
    return {"x": x, "w_in": w_in, "b_f": b_f, "conv_w": conv_w, "pool_w": pool_w,
            "pool_scale": pool_scale, "w_out": w_out, "ln_g": ln_g, "ln_b": ln_b}


def reference(x, w_in, b_f, conv_w, pool_w, pool_scale, w_out, ln_g, ln_b):
    b, s, _ = x.shape
    for l in range(DEPTH):
        proj = jnp.einsum('bsd,de->bse', x, w_in[l])
        (q, k, v, g_att, fg, c_b, c_c, c_h, g_conv, p_u, g_pool) = jnp.split(
            proj, IN_SPLITS, axis=-1)
        heads = (b, s, N_ATT_HEADS, HEAD_DIM)
        y_att = forgetting_attention(q.reshape(heads), k.reshape(heads), v.reshape(heads),
                                     fg, b_f[l]) * jax.nn.silu(g_att)
        y_conv = short_conv_mixer(c_b, c_c, c_h, conv_w[l]) * jax.nn.silu(g_conv)
        y_pool = multiscale_pool_mixer(p_u, pool_w[l], pool_scale[l]) * jax.nn.silu(g_pool)
        y = jnp.concatenate([y_att, y_conv, y_pool], axis=-1)
        y = jnp.einsum('bse,ed->bsd', y, w_out[l])
        x = layer_norm(DEEPNORM_ALPHA * x + y, ln_g[l], ln_b[l])
    return x
```

```python
import functools

import jax
import jax.numpy as jnp
from jax import lax
from jax.experimental import pallas as pl
from jax.experimental.pallas import tpu as pltpu

F32 = jnp.float32
BF16 = jnp.bfloat16

HEAD_DIM = 128
CONV_TAPS = 3
POOL_WINDOWS = (2, 4, 8, 16)
LN_EPS = 1e-5
LANES = 128
HALO = 16

VMEM_LIMIT = 56 * 1024 * 1024


def _silu(x):
    return x / (1.0 + jnp.exp(-x))


def _inproj_kernel(x_ref, w_ref, wfg_ref, o_ref, fg_ref):
    x = x_ref[...]
    o_ref[...] = jnp.dot(x, w_ref[...], preferred_element_type=F32).astype(o_ref.dtype)

    @pl.when(pl.program_id(1) == 0)
    def _():
        fg_ref[...] = jnp.dot(x, wfg_ref[...], preferred_element_type=F32)


def _inproj(x, w, wfg, *, tm, tn):
    m, d = x.shape
    n = w.shape[1]
    return pl.pallas_call(
        _inproj_kernel,
        grid=(m // tm, n // tn),
        in_specs=[
            pl.BlockSpec((tm, d), lambda i, j: (i, 0)),
            pl.BlockSpec((d, tn), lambda i, j: (0, j)),
            pl.BlockSpec((d, LANES), lambda i, j: (0, 0)),
        ],
        out_specs=[
            pl.BlockSpec((tm, tn), lambda i, j: (i, j)),
            pl.BlockSpec((tm, LANES), lambda i, j: (i, 0)),
        ],
        out_shape=[
            jax.ShapeDtypeStruct((m, n), BF16),
            jax.ShapeDtypeStruct((m, LANES), F32),
        ],
        compiler_params=pltpu.CompilerParams(
            dimension_semantics=("parallel", "arbitrary"),
            vmem_limit_bytes=VMEM_LIMIT),
        name="inproj",
    )(x, w, wfg)


def _cum_kernel(fg_ref, bf_ref, o_ref):
    z = fg_ref[0] + bf_ref[...]
    lf = jnp.minimum(z, 0.0) - jnp.log1p(jnp.exp(-jnp.abs(z)))
    s = lf.shape[1]
    lane = lax.broadcasted_iota(jnp.int32, lf.shape, 1)
    k = 1
    while k < s:
        lf = lf + jnp.where(lane >= k, pltpu.roll(lf, k, axis=1), 0.0)
        k *= 2
    o_ref[0] = lf


def _cum_log_forget(fg_t, b_f):
    b, h, s = fg_t.shape
    return pl.pallas_call(
        _cum_kernel,
        grid=(b,),
        in_specs=[
            pl.BlockSpec((1, h, s), lambda i: (i, 0, 0)),
            pl.BlockSpec((h, 1), lambda i: (0, 0)),
        ],
        out_specs=pl.BlockSpec((1, h, s), lambda i: (i, 0, 0)),
        out_shape=jax.ShapeDtypeStruct((b, h, s), F32),
        compiler_params=pltpu.CompilerParams(dimension_semantics=("parallel",)),
        name="cum_log_forget",
    )(fg_t, b_f.reshape(h, 1))


def _att_kernel(q_ref, k_ref, v_ref, g_ref, c_ref, o_ref, *, tq, scale):
    s_len = q_ref.shape[0]
    nq = s_len // tq
    row = lax.broadcasted_iota(jnp.int32, (tq, tq), 0)
    col = lax.broadcasted_iota(jnp.int32, (tq, tq), 1)
    causal = col <= row

    def q_block(qi, _):
        qs = pl.multiple_of(qi * tq, tq)
        q = q_ref[pl.ds(qs, tq), :]
        c0 = c_ref[0, :, pl.ds(qs, tq)][:, 0:1]

        def kv_step(kj, carry, masked):
            m, l, acc = carry
            ks = pl.multiple_of(kj * tq, tq)
            kb = k_ref[pl.ds(ks, tq), :]
            vb = v_ref[pl.ds(ks, tq), :]
            s = lax.dot_general(q, kb, (((1,), (1,)), ((), ())),
                                preferred_element_type=F32) * scale
            s = s + (c0 - c_ref[0, :, pl.ds(ks, tq)])
            if masked:
                s = jnp.where(causal, s, -jnp.inf)
            m_new = jnp.maximum(m, jnp.max(s, axis=-1, keepdims=True))
            alpha = jnp.exp(m - m_new)
            p = jnp.exp(s - m_new)
            l = alpha * l + jnp.sum(p, axis=-1, keepdims=True)
            acc = alpha * acc + jnp.dot(p.astype(BF16), vb, preferred_element_type=F32)
            return m_new, l, acc

        init = (jnp.full((tq, 1), -jnp.inf, F32), jnp.zeros((tq, 1), F32),
                jnp.zeros((tq, HEAD_DIM), F32))
        carry = lax.fori_loop(0, qi, functools.partial(kv_step, masked=False), init)
        _, l, acc = kv_step(qi, carry, masked=True)
        gate = _silu(g_ref[pl.ds(qs, tq), :].astype(F32))
        o_ref[pl.ds(qs, tq), :] = (acc / l * gate).astype(o_ref.dtype)
        return 0

    lax.fori_loop(0, nq, q_block, 0)


def _attention(proj, cum, *, batch, seq, n_heads, tq):
    m = proj.shape[0]
    h = n_heads
    blk = lambda off: pl.BlockSpec((seq, HEAD_DIM), lambda b, i: (b, off + i))
    return pl.pallas_call(
        functools.partial(_att_kernel, tq=tq, scale=HEAD_DIM ** -0.5),
        grid=(batch, h),
        in_specs=[blk(0), blk(h), blk(2 * h), blk(3 * h),
                  pl.BlockSpec((1, 1, seq), lambda b, i: (b * h + i, 0, 0))],
        out_specs=pl.BlockSpec((seq, HEAD_DIM), lambda b, i: (b, i)),
        out_shape=jax.ShapeDtypeStruct((m, h * HEAD_DIM), BF16),
        compiler_params=pltpu.CompilerParams(
            dimension_semantics=("parallel", "parallel"),
            vmem_limit_bytes=VMEM_LIMIT),
        name="fox_attention",
    )(proj, proj, proj, proj, cum.reshape(batch * h, 1, seq))


def _mixer_kernel(cb_ref, cc_ref, ch_ref, gc_ref, pu_ref, gp_ref,
                  cch_ref, chh_ref, puh_ref, cw_ref, pw_ref, ps_ref,
                  oc_ref, op_ref, *, tiles_per_seq):
    ts = cb_ref.shape[0]
    t_idx = pl.program_id(0) % tiles_per_seq
    keep = (t_idx > 0).astype(F32)

    def shifted(ext, k):
        return pltpu.roll(ext, k, axis=0)[HALO:]

    u = cc_ref[...].astype(F32) * ch_ref[...].astype(F32)
    u_halo = cch_ref[...].astype(F32) * chh_ref[...].astype(F32) * keep
    ext = jnp.concatenate([u_halo, u], axis=0)
    cw = cw_ref[...]
    y = cw[0:1] * shifted(ext, 2) + cw[1:2] * shifted(ext, 1) + cw[2:3] * u
    y = cb_ref[...].astype(F32) * y
    oc_ref[...] = (y * _silu(gc_ref[...].astype(F32))).astype(oc_ref.dtype)

    p = pu_ref[...].astype(F32)
    pext = jnp.concatenate([puh_ref[...].astype(F32) * keep, p], axis=0)
    pos = t_idx * ts + lax.broadcasted_iota(jnp.int32, (ts, 1), 0)
    pg = pw_ref.shape[1]
    outs = []
    for g, w in enumerate(POOL_WINDOWS):
        acc = pext[:, g * pg:(g + 1) * pg]
        k = 1
        while k < w:
            acc = acc + pltpu.roll(acc, k, axis=0)
            k *= 2
        count = jnp.minimum(pos + 1, w).astype(F32)
        z = acc[HALO:] / count - p[:, g * pg:(g + 1) * pg]
        outs.append(jnp.dot(z.astype(BF16), pw_ref[g], preferred_element_type=F32))
    yp = jnp.concatenate(outs, axis=-1) * ps_ref[...]
    op_ref[...] = (yp * _silu(gp_ref[...].astype(F32))).astype(op_ref.dtype)


def _mixers(proj, conv_w, pool_w, pool_scale, *, seq, col0, width, ts):
    m = proj.shape[0]
    c0 = col0 // width
    cur = lambda j: pl.BlockSpec((ts, width), lambda i: (i, c0 + j))
    hpt = ts // HALO
    halo = lambda j: pl.BlockSpec(
        (HALO, width), lambda i: (jnp.maximum(i * hpt - 1, 0), c0 + j))
    full = lambda a: pl.BlockSpec(a.shape, lambda i: (0,) * a.ndim)
    ps = pool_scale.reshape(1, width)
    return pl.pallas_call(
        functools.partial(_mixer_kernel, tiles_per_seq=seq // ts),
        grid=(m // ts,),
        in_specs=[cur(0), cur(1), cur(2), cur(3), cur(4), cur(5),
                  halo(1), halo(2), halo(4), full(conv_w), full(pool_w), full(ps)],
        out_specs=[pl.BlockSpec((ts, width), lambda i: (i, 0))] * 2,
        out_shape=[jax.ShapeDtypeStruct((m, width), BF16)] * 2,
        compiler_params=pltpu.CompilerParams(
            dimension_semantics=("parallel",), vmem_limit_bytes=VMEM_LIMIT),
        name="conv_pool_mixers",
    )(proj, proj, proj, proj, proj, proj, proj, proj, proj, conv_w, pool_w, ps)


def _outproj_kernel(ya_ref, yc_ref, yp_ref, x_ref, wa_ref, wc_ref, wp_ref, g_ref, b_ref,
                    o_ref, ob_ref, *, alpha):
    y = jnp.dot(ya_ref[...], wa_ref[...], preferred_element_type=F32)
    y = y + jnp.dot(yc_ref[...], wc_ref[...], preferred_element_type=F32)
    y = y + jnp.dot(yp_ref[...], wp_ref[...], preferred_element_type=F32)
    r = alpha * x_ref[...] + y
    mu = jnp.mean(r, axis=-1, keepdims=True)
    d = r - mu
    var = jnp.mean(d * d, axis=-1, keepdims=True)
    out = d * lax.rsqrt(var + LN_EPS) * g_ref[...] + b_ref[...]
    o_ref[...] = out
    ob_ref[...] = out.astype(ob_ref.dtype)


def _outproj(ya, yc, yp, x, wa, wc, wp, ln_g, ln_b, *, alpha, tm):
    m, d = x.shape
    rows = lambda a: pl.BlockSpec((tm, a.shape[1]), lambda i: (i, 0))
    full = lambda a: pl.BlockSpec(a.shape, lambda i: (0, 0))
    g2, b2 = ln_g.reshape(1, d), ln_b.reshape(1, d)
    return pl.pallas_call(
        functools.partial(_outproj_kernel, alpha=alpha),
        grid=(m // tm,),
        in_specs=[rows(ya), rows(yc), rows(yp), rows(x),
                  full(wa), full(wc), full(wp), full(g2), full(b2)],
        out_specs=[pl.BlockSpec((tm, d), lambda i: (i, 0))] * 2,
        out_shape=[jax.ShapeDtypeStruct((m, d), F32), jax.ShapeDtypeStruct((m, d), BF16)],
        compiler_params=pltpu.CompilerParams(
            dimension_semantics=("parallel",), vmem_limit_bytes=VMEM_LIMIT),
        name="outproj_deepnorm",
    )(ya, yc, yp, x, wa, wc, wp, g2, b2)


def kernel(x, w_in, b_f, conv_w, pool_w, pool_scale, w_out, ln_g, ln_b):
    batch, seq, d = x.shape
    depth = w_in.shape[0]
    n_heads = b_f.shape[1]
    att_w = n_heads * HEAD_DIM
    conv_width = conv_w.shape[2]
    pool_width = pool_scale.shape[1]
    assert conv_width == pool_width
    fg0 = 4 * att_w
    alpha = (2 * depth) ** 0.25
    m = batch * seq

    xf = x.reshape(m, d)
    xb = xf.astype(BF16)
    for l in range(depth):
        w = w_in[l]
        w_main = jnp.concatenate([w[:, :fg0], w[:, fg0 + n_heads:]], axis=1).astype(BF16)
        w_fg = jnp.pad(w[:, fg0:fg0 + n_heads], ((0, 0), (0, LANES - n_heads))).astype(BF16)
        proj, fg = _inproj(xb, w_main, w_fg, tm=1024, tn=1024)

        fg_t = fg[:, :n_heads].reshape(batch, seq, n_heads).transpose(0, 2, 1)
        cum = _cum_log_forget(fg_t, b_f[l])
        y_att = _attention(proj, cum, batch=batch, seq=seq, n_heads=n_heads, tq=512)

        y_conv, y_pool = _mixers(proj, conv_w[l], pool_w[l].astype(BF16), pool_scale[l],
                                 seq=seq, col0=fg0, width=conv_width, ts=512)

        wo = w_out[l].astype(BF16)
        xf, xb = _outproj(y_att, y_conv, y_pool, xf,
                          wo[:att_w], wo[att_w:att_w + conv_width], wo[att_w + conv_width:],
                          ln_g[l], ln_b[l], alpha=alpha, tm=512)
    return xf.reshape(batch, seq, d)
```

```python
import functools
import math

import jax
import jax.numpy as jnp
from jax import lax
from jax.experimental import pallas as pl
from jax.experimental.pallas import tpu as pltpu

F32 = jnp.float32
BF16 = jnp.bfloat16

HEAD_DIM = 128
CONV_TAPS = 3
POOL_WINDOWS = (2, 4, 8, 16)
LN_EPS = 1e-5
LANES = 128
HALO = 16
LOG2E = math.log2(math.e)

VMEM_LIMIT = 56 * 1024 * 1024


def _silu(x):
    return x / (1.0 + jnp.exp(-x))


def _inproj_kernel(x_ref, w_ref, cs_ref, wfg_ref, o_ref, fg_ref):
    x = x_ref[...]
    acc = jnp.dot(x, w_ref[...], preferred_element_type=F32)
    o_ref[...] = (acc * cs_ref[...]).astype(o_ref.dtype)

    @pl.when(pl.program_id(1) == 0)
    def _():
        fg_ref[...] = jnp.dot(x, wfg_ref[...], preferred_element_type=F32)


def _inproj(x, w, col_scale, wfg, *, tm, tn):
    m, d = x.shape
    n = w.shape[1]
    return pl.pallas_call(
        _inproj_kernel,
        grid=(m // tm, n // tn),
        in_specs=[
            pl.BlockSpec((tm, d), lambda i, j: (i, 0)),
            pl.BlockSpec((d, tn), lambda i, j: (0, j)),
            pl.BlockSpec((1, tn), lambda i, j: (0, j)),
            pl.BlockSpec((d, LANES), lambda i, j: (0, 0)),
        ],
        out_specs=[
            pl.BlockSpec((tm, tn), lambda i, j: (i, j)),
            pl.BlockSpec((tm, LANES), lambda i, j: (i, 0)),
        ],
        out_shape=[
            jax.ShapeDtypeStruct((m, n), BF16),
            jax.ShapeDtypeStruct((m, LANES), F32),
        ],
        compiler_params=pltpu.CompilerParams(
            dimension_semantics=("parallel", "arbitrary"),
            vmem_limit_bytes=VMEM_LIMIT),
        name="inproj",
    )(x, w, col_scale, wfg)


def _cum_kernel(fg_ref, bf_ref, o_ref):
    z = fg_ref[0] + bf_ref[...]
    lf = jnp.minimum(z, 0.0) - jnp.log1p(jnp.exp(-jnp.abs(z)))
    s = lf.shape[1]
    lane = lax.broadcasted_iota(jnp.int32, lf.shape, 1)
    k = 1
    while k < s:
        lf = lf + jnp.where(lane >= k, pltpu.roll(lf, k, axis=1), 0.0)
        k *= 2
    o_ref[0] = lf


def _cum_log_forget(fg_t, b_f):
    b, h, s = fg_t.shape
    return pl.pallas_call(
        _cum_kernel,
        grid=(b,),
        in_specs=[
            pl.BlockSpec((1, h, s), lambda i: (i, 0, 0)),
            pl.BlockSpec((h, 1), lambda i: (0, 0)),
        ],
        out_specs=pl.BlockSpec((1, h, s), lambda i: (i, 0, 0)),
        out_shape=jax.ShapeDtypeStruct((b, h, s), F32),
        compiler_params=pltpu.CompilerParams(dimension_semantics=("parallel",)),
        name="cum_log_forget",
    )(fg_t, b_f.reshape(h, 1))


def _att_kernel(q_ref, k_ref, v_ref, g_ref, c_ref, o_ref,
                vt_ref, ckb_ref, s_scr, m_scr, l_scr, acc_scr, *, tq, heads, cw):
    s_len = q_ref.shape[0]
    nq = s_len // tq
    groups = s_len // LANES
    krow = lax.broadcasted_iota(jnp.int32, (tq, tq), 0)
    qcol = lax.broadcasted_iota(jnp.int32, (tq, tq), 1)
    causal = krow <= qcol

    for h in range(heads):
        lo = h * HEAD_DIM

        def transpose_v(i, _, h=h, lo=lo):
            rs = pl.multiple_of(i * tq, tq)
            vt_ref[h * nq + i] = v_ref[pl.ds(rs, tq), lo:lo + HEAD_DIM].astype(F32).T.astype(BF16)
            return 0

        lax.fori_loop(0, nq, transpose_v, 0)
        c = c_ref[h] * (-LOG2E)
        ct = jnp.concatenate([c, jnp.zeros((LANES - groups, LANES), F32)], axis=0).T
        for j in range(groups):
            ckb_ref[h, j * LANES:(j + 1) * LANES, :] = jnp.broadcast_to(ct[:, j:j + 1], (LANES, LANES))

    chains = [(h, j) for h in range(heads) for j in range(tq // cw)]
    n = len(chains)

    def q_block(qi, _):
        qs = pl.multiple_of(qi * tq, tq)

        def scores(c, kj, diag):
            h, j = chains[c]
            cols = slice(h * HEAD_DIM, (h + 1) * HEAD_DIM)
            ks = pl.multiple_of(kj * tq, tq)
            s = lax.dot_general(k_ref[pl.ds(ks, tq), cols], q_ref[pl.ds(qs + j * cw, cw), cols],
                                (((1,), (1,)), ((), ())), preferred_element_type=F32)
            s = s + jnp.concatenate([ckb_ref[h, pl.ds(ks, tq), :]] * (cw // LANES), axis=1)
            if diag:
                s = jnp.where(causal[:, j * cw:(j + 1) * cw], s, -jnp.inf)
            s_scr[c] = s

        def softmax(c):
            m = m_scr[c]
            m_new = jnp.maximum(m, jnp.max(s_scr[c], axis=0, keepdims=True))
            alpha = jnp.exp2(m - m_new)
            p = jnp.exp2(s_scr[c] - m_new)
            l_scr[c] = alpha * l_scr[c] + jnp.sum(p, axis=0, keepdims=True)
            m_scr[c] = m_new
            return alpha, p.astype(BF16)

        def values(c, kj, alpha, p):
            h, _ = chains[c]
            acc_scr[c] = alpha * acc_scr[c] + jnp.dot(vt_ref[h * nq + kj], p,
                                                      preferred_element_type=F32)

        def kv_step(kj, nxt):
            st = [None] * n
            for t in range(n + 1):
                if t < n:
                    st[t] = softmax(t)
                if t >= 1:
                    values(t - 1, kj, *st[t - 1])
                    if nxt is not None:
                        scores(t - 1, kj + 1, diag=(nxt == "diag"))

        for c in range(n):
            m_scr[c] = jnp.full((1, cw), -jnp.inf, F32)
            l_scr[c] = jnp.zeros((1, cw), F32)
            acc_scr[c] = jnp.zeros((HEAD_DIM, cw), F32)

        @pl.when(qi == 0)
        def _():
            for c in range(n):
                scores(c, 0, diag=True)

        @pl.when(qi > 0)
        def _():
            for c in range(n):
                scores(c, 0, diag=False)

        def body(kj, _):
            kv_step(kj, "plain")
            return 0

        lax.fori_loop(0, qi - 1, body, 0)

        @pl.when(qi > 0)
        def _():
            kv_step(qi - 1, "diag")

        kv_step(qi, None)
        for c, (h, j) in enumerate(chains):
            rows = pl.ds(qs + j * cw, cw)
            cols = slice(h * HEAD_DIM, (h + 1) * HEAD_DIM)
            gate = _silu(g_ref[rows, cols].astype(F32))
            o_ref[rows, cols] = ((acc_scr[c] / l_scr[c]).T * gate).astype(o_ref.dtype)
        return 0

    lax.fori_loop(0, nq, q_block, 0)


def _attention(proj, cum, *, batch, seq, n_heads, tq, heads, cw):
    m = proj.shape[0]
    hb = n_heads // heads
    width = heads * HEAD_DIM
    n_chains = heads * (tq // cw)
    blk = lambda off: pl.BlockSpec((seq, width), lambda b, i: (b, off + i))
    cum3 = cum.reshape(batch * n_heads, seq // LANES, LANES)
    return pl.pallas_call(
        functools.partial(_att_kernel, tq=tq, heads=heads, cw=cw),
        grid=(batch, hb),
        in_specs=[blk(0), blk(hb), blk(2 * hb), blk(3 * hb),
                  pl.BlockSpec((heads, seq // LANES, LANES), lambda b, i: (b * hb + i, 0, 0))],
        out_specs=pl.BlockSpec((seq, width), lambda b, i: (b, i)),
        out_shape=jax.ShapeDtypeStruct((m, n_heads * HEAD_DIM), BF16),
        scratch_shapes=[pltpu.VMEM((heads * (seq // tq), HEAD_DIM, tq), BF16),
                        pltpu.VMEM((heads, seq, LANES), F32),
                        pltpu.VMEM((n_chains, tq, cw), F32),
                        pltpu.VMEM((n_chains, 1, cw), F32),
                        pltpu.VMEM((n_chains, 1, cw), F32),
                        pltpu.VMEM((n_chains, HEAD_DIM, cw), F32)],
        compiler_params=pltpu.CompilerParams(
            dimension_semantics=("parallel", "parallel"),
            vmem_limit_bytes=VMEM_LIMIT),
        name="fox_attention",
    )(proj, proj, proj, proj, cum3)


def _mixer_kernel(cb_ref, cc_ref, ch_ref, gc_ref, pu_ref, gp_ref,
                  cch_ref, chh_ref, puh_ref, cw_ref, pw_ref, ps_ref,
                  oc_ref, op_ref, *, tiles_per_seq):
    ts = cb_ref.shape[0]
    t_idx = pl.program_id(0) % tiles_per_seq
    keep = (t_idx > 0).astype(F32)

    def shifted(ext, k):
        return pltpu.roll(ext, k, axis=0)[HALO:]

    u = cc_ref[...].astype(F32) * ch_ref[...].astype(F32)
    u_halo = cch_ref[...].astype(F32) * chh_ref[...].astype(F32) * keep
    ext = jnp.concatenate([u_halo, u], axis=0)
    cw = cw_ref[...]
    y = cw[0:1] * shifted(ext, 2) + cw[1:2] * shifted(ext, 1) + cw[2:3] * u
    y = cb_ref[...].astype(F32) * y
    oc_ref[...] = (y * _silu(gc_ref[...].astype(F32))).astype(oc_ref.dtype)

    p = pu_ref[...].astype(F32)
    pext = jnp.concatenate([puh_ref[...].astype(F32) * keep, p], axis=0)
    pos = t_idx * ts + lax.broadcasted_iota(jnp.int32, (ts, 1), 0)
    pg = pw_ref.shape[1]
    outs = []
    for g, w in enumerate(POOL_WINDOWS):
        acc = pext[:, g * pg:(g + 1) * pg]
        k = 1
        while k < w:
            acc = acc + pltpu.roll(acc, k, axis=0)
            k *= 2
        count = jnp.minimum(pos + 1, w).astype(F32)
        z = acc[HALO:] / count - p[:, g * pg:(g + 1) * pg]
        outs.append(jnp.dot(z.astype(BF16), pw_ref[g], preferred_element_type=F32))
    yp = jnp.concatenate(outs, axis=-1) * ps_ref[...]
    op_ref[...] = (yp * _silu(gp_ref[...].astype(F32))).astype(op_ref.dtype)


def _mixers(proj, conv_w, pool_w, pool_scale, *, seq, col0, width, ts):
    m = proj.shape[0]
    c0 = col0 // width
    cur = lambda j: pl.BlockSpec((ts, width), lambda i: (i, c0 + j))
    hpt = ts // HALO
    halo = lambda j: pl.BlockSpec(
        (HALO, width), lambda i: (jnp.maximum(i * hpt - 1, 0), c0 + j))
    full = lambda a: pl.BlockSpec(a.shape, lambda i: (0,) * a.ndim)
    ps = pool_scale.reshape(1, width)
    return pl.pallas_call(
        functools.partial(_mixer_kernel, tiles_per_seq=seq // ts),
        grid=(m // ts,),
        in_specs=[cur(0), cur(1), cur(2), cur(3), cur(4), cur(5),
                  halo(1), halo(2), halo(4), full(conv_w), full(pool_w), full(ps)],
        out_specs=[pl.BlockSpec((ts, width), lambda i: (i, 0))] * 2,
        out_shape=[jax.ShapeDtypeStruct((m, width), BF16)] * 2,
        compiler_params=pltpu.CompilerParams(
            dimension_semantics=("parallel",), vmem_limit_bytes=VMEM_LIMIT),
        name="conv_pool_mixers",
    )(proj, proj, proj, proj, proj, proj, proj, proj, proj, conv_w, pool_w, ps)


def _outproj_kernel(ya_ref, yc_ref, yp_ref, x_ref, wa_ref, wc_ref, wp_ref, g_ref, b_ref,
                    o_ref, ob_ref, *, alpha, rc):
    for c in range(x_ref.shape[0] // rc):
        rows = slice(c * rc, (c + 1) * rc)
        y = jnp.dot(ya_ref[rows, :], wa_ref[...], preferred_element_type=F32)
        y = y + jnp.dot(yc_ref[rows, :], wc_ref[...], preferred_element_type=F32)
        y = y + jnp.dot(yp_ref[rows, :], wp_ref[...], preferred_element_type=F32)
        r = alpha * x_ref[rows, :] + y
        mu = jnp.mean(r, axis=-1, keepdims=True)
        d = r - mu
        var = jnp.mean(d * d, axis=-1, keepdims=True)
        out = d * lax.rsqrt(var + LN_EPS) * g_ref[...] + b_ref[...]
        o_ref[rows, :] = out
        ob_ref[rows, :] = out.astype(ob_ref.dtype)


def _outproj(ya, yc, yp, x, wa, wc, wp, ln_g, ln_b, *, alpha, tm, rc):
    m, d = x.shape
    rows = lambda a: pl.BlockSpec((tm, a.shape[1]), lambda i: (i, 0))
    full = lambda a: pl.BlockSpec(a.shape, lambda i: (0, 0))
    g2, b2 = ln_g.reshape(1, d), ln_b.reshape(1, d)
    return pl.pallas_call(
        functools.partial(_outproj_kernel, alpha=alpha, rc=rc),
        grid=(m // tm,),
        in_specs=[rows(ya), rows(yc), rows(yp), rows(x),
                  full(wa), full(wc), full(wp), full(g2), full(b2)],
        out_specs=[pl.BlockSpec((tm, d), lambda i: (i, 0))] * 2,
        out_shape=[jax.ShapeDtypeStruct((m, d), F32), jax.ShapeDtypeStruct((m, d), BF16)],
        compiler_params=pltpu.CompilerParams(
            dimension_semantics=("parallel",), vmem_limit_bytes=VMEM_LIMIT),
        name="outproj_deepnorm",
    )(ya, yc, yp, x, wa, wc, wp, g2, b2)


def kernel(x, w_in, b_f, conv_w, pool_w, pool_scale, w_out, ln_g, ln_b):
    batch, seq, d = x.shape
    depth = w_in.shape[0]
    n_heads = b_f.shape[1]
    att_w = n_heads * HEAD_DIM
    conv_width = conv_w.shape[2]
    pool_width = pool_scale.shape[1]
    assert conv_width == pool_width
    fg0 = 4 * att_w
    n_main = w_in.shape[2] - n_heads
    alpha = (2 * depth) ** 0.25
    m = batch * seq
    col_scale = jnp.where(jnp.arange(n_main) < att_w, HEAD_DIM ** -0.5 * LOG2E, 1.0)
    col_scale = col_scale.astype(F32).reshape(1, n_main)

    xf = x.reshape(m, d)
    xb = xf.astype(BF16)
    for l in range(depth):
        w = w_in[l]
        w_main = jnp.concatenate([w[:, :fg0], w[:, fg0 + n_heads:]], axis=1).astype(BF16)
        w_fg = jnp.pad(w[:, fg0:fg0 + n_heads], ((0, 0), (0, LANES - n_heads))).astype(BF16)
        proj, fg = _inproj(xb, w_main, col_scale, w_fg, tm=1024, tn=1024)

        fg_t = fg[:, :n_heads].reshape(batch, seq, n_heads).transpose(0, 2, 1)
        cum = _cum_log_forget(fg_t, b_f[l])
        y_att = _attention(proj, cum, batch=batch, seq=seq, n_heads=n_heads, tq=512, heads=2, cw=256)

        y_conv, y_pool = _mixers(proj, conv_w[l], pool_w[l].astype(BF16), pool_scale[l],
                                 seq=seq, col0=fg0, width=conv_width, ts=512)

        wo = w_out[l].astype(BF16)
        xf, xb = _outproj(y_att, y_conv, y_pool, xf,
                          wo[:att_w], wo[att_w:att_w + conv_width], wo[att_w + conv_width:],
                          ln_g[l], ln_b[l], alpha=alpha, tm=512, rc=256)
    return xf.reshape(batch, seq, d)
```

```python
import functools
import math

import jax
import jax.numpy as jnp
from jax import lax
from jax.experimental import pallas as pl
from jax.experimental.pallas import tpu as pltpu

F32 = jnp.float32
BF16 = jnp.bfloat16

HEAD_DIM = 128
CONV_TAPS = 3
POOL_WINDOWS = (2, 4, 8, 16)
LN_EPS = 1e-5
LANES = 128
HALO = 16
LOG2E = math.log2(math.e)

VMEM_LIMIT = 56 * 1024 * 1024


def _silu(x):
    return x / (1.0 + jnp.exp(-x))


def _inproj_kernel(x_ref, w_ref, cs_ref, wfg_ref, o_ref, fg_ref, *xb_scr):
    xb_ref = xb_scr[0] if xb_scr else x_ref

    @pl.when(pl.program_id(1) == 0)
    def _():
        if xb_scr:
            xb_ref[...] = x_ref[...].astype(BF16)
        fg_ref[...] = jnp.dot(xb_ref[...], wfg_ref[...], preferred_element_type=F32)

    acc = jnp.dot(xb_ref[...], w_ref[...], preferred_element_type=F32)
    o_ref[...] = (acc * cs_ref[...]).astype(o_ref.dtype)


def _inproj(x, w, col_scale, wfg, *, tm, tn):
    m, d = x.shape
    n = w.shape[1]
    scratch = [] if x.dtype == BF16 else [pltpu.VMEM((tm, d), BF16)]
    return pl.pallas_call(
        _inproj_kernel,
        grid=(m // tm, n // tn),
        in_specs=[
            pl.BlockSpec((tm, d), lambda i, j: (i, 0)),
            pl.BlockSpec((d, tn), lambda i, j: (0, j)),
            pl.BlockSpec((1, tn), lambda i, j: (0, j)),
            pl.BlockSpec((d, LANES), lambda i, j: (0, 0)),
        ],
        out_specs=[
            pl.BlockSpec((tm, tn), lambda i, j: (i, j)),
            pl.BlockSpec((tm, LANES), lambda i, j: (i, 0)),
        ],
        out_shape=[
            jax.ShapeDtypeStruct((m, n), BF16),
            jax.ShapeDtypeStruct((m, LANES), F32),
        ],
        scratch_shapes=scratch,
        compiler_params=pltpu.CompilerParams(
            dimension_semantics=("parallel", "arbitrary"),
            vmem_limit_bytes=VMEM_LIMIT),
        name="inproj",
    )(x, w, col_scale, wfg)


def _cum_kernel(fg_ref, bf_ref, o_ref):
    z = fg_ref[0] + bf_ref[...]
    lf = jnp.minimum(z, 0.0) - jnp.log1p(jnp.exp(-jnp.abs(z)))
    s = lf.shape[1]
    lane = lax.broadcasted_iota(jnp.int32, lf.shape, 1)
    k = 1
    while k < s:
        lf = lf + jnp.where(lane >= k, pltpu.roll(lf, k, axis=1), 0.0)
        k *= 2
    o_ref[0] = lf


def _cum_log_forget(fg_t, b_f):
    b, h, s = fg_t.shape
    return pl.pallas_call(
        _cum_kernel,
        grid=(b,),
        in_specs=[
            pl.BlockSpec((1, h, s), lambda i: (i, 0, 0)),
            pl.BlockSpec((h, 1), lambda i: (0, 0)),
        ],
        out_specs=pl.BlockSpec((1, h, s), lambda i: (i, 0, 0)),
        out_shape=jax.ShapeDtypeStruct((b, h, s), F32),
        compiler_params=pltpu.CompilerParams(dimension_semantics=("parallel",)),
        name="cum_log_forget",
    )(fg_t, b_f.reshape(h, 1))


def _att_kernel(q_ref, k_ref, v_ref, g_ref, c_ref, o_ref,
                vt_ref, ckb_ref, m_scr, l_scr, acc_scr, *s_scr, tq, heads, cw):
    s_len = q_ref.shape[0]
    nq = s_len // tq
    groups = s_len // LANES
    krow = lax.broadcasted_iota(jnp.int32, (tq, tq), 0)
    qcol = lax.broadcasted_iota(jnp.int32, (tq, tq), 1)
    causal = krow <= qcol

    for h in range(heads):
        lo = h * HEAD_DIM

        def transpose_v(i, _, h=h, lo=lo):
            rs = pl.multiple_of(i * tq, tq)
            vt_ref[h * nq + i] = v_ref[pl.ds(rs, tq), lo:lo + HEAD_DIM].astype(F32).T.astype(BF16)
            return 0

        lax.fori_loop(0, nq, transpose_v, 0)
        c = c_ref[h] * (-LOG2E)
        ct = jnp.concatenate([c, jnp.zeros((LANES - groups, LANES), F32)], axis=0).T
        for j in range(groups):
            ckb_ref[h, j * LANES:(j + 1) * LANES, :] = jnp.broadcast_to(ct[:, j:j + 1], (LANES, LANES))

    chains = [(h, j) for h in range(heads) for j in range(tq // cw)]
    n = len(chains)

    def scores(c, qi, kj, diag):
        h, j = chains[c]
        cols = slice(h * HEAD_DIM, (h + 1) * HEAD_DIM)
        qs = pl.multiple_of(qi * tq + j * cw, cw)
        ks = pl.multiple_of(kj * tq, tq)
        s = lax.dot_general(k_ref[pl.ds(ks, tq), cols], q_ref[pl.ds(qs, cw), cols],
                            (((1,), (1,)), ((), ())), preferred_element_type=F32)
        s = s + jnp.concatenate([ckb_ref[h, pl.ds(ks, tq), :]] * (cw // LANES), axis=1)
        if diag:
            s = jnp.where(causal[:, j * cw:(j + 1) * cw], s, -jnp.inf)
        return s

    def softmax(c, slot):
        m = m_scr[c]
        m_new = jnp.maximum(m, jnp.max(s_scr[c][slot], axis=0, keepdims=True))
        alpha = jnp.exp2(m - m_new)
        p = jnp.exp2(s_scr[c][slot] - m_new)
        l_scr[c] = alpha * l_scr[c] + jnp.sum(p, axis=0, keepdims=True)
        m_scr[c] = m_new
        return alpha, p.astype(BF16)

    def values(c, kj, alpha, p):
        h, _ = chains[c]
        acc_scr[c] = alpha * acc_scr[c] + jnp.dot(vt_ref[h * nq + kj], p,
                                                  preferred_element_type=F32)

    def tile_step(qi, kj, slot, nxt):
        for c in range(n):
            alpha, p = softmax(c, slot)
            s_next = scores(c, *nxt)
            values(c, kj, alpha, p)
            s_scr[c][1 - slot] = s_next
        return 1 - slot

    for c in range(n):
        s_scr[c][0] = scores(c, 0, 0, True)

    def q_block(qi, slot):
        for c in range(n):
            m_scr[c] = jnp.full((1, cw), -jnp.inf, F32)
            l_scr[c] = jnp.zeros((1, cw), F32)
            acc_scr[c] = jnp.zeros((HEAD_DIM, cw), F32)

        slot = lax.fori_loop(
            0, qi - 1, lambda kj, sl: tile_step(qi, kj, sl, (qi, kj + 1, False)), slot)
        slot = lax.cond(qi > 0, lambda sl: tile_step(qi, qi - 1, sl, (qi, qi, True)),
                        lambda sl: sl, slot)
        slot = tile_step(qi, qi, slot, (jnp.minimum(qi + 1, nq - 1), 0, False))
        for c, (h, j) in enumerate(chains):
            rows = pl.ds(pl.multiple_of(qi * tq + j * cw, cw), cw)
            cols = slice(h * HEAD_DIM, (h + 1) * HEAD_DIM)
            gate = _silu(g_ref[rows, cols].astype(F32))
            o_ref[rows, cols] = ((acc_scr[c] / l_scr[c]).T * gate).astype(o_ref.dtype)
        return slot

    lax.fori_loop(0, nq, q_block, 0)


def _attention(proj, cum, *, batch, seq, n_heads, tq, heads, cw):
    m = proj.shape[0]
    hb = n_heads // heads
    width = heads * HEAD_DIM
    n_chains = heads * (tq // cw)
    blk = lambda off: pl.BlockSpec((seq, width), lambda b, i: (b, off + i))
    cum3 = cum.reshape(batch * n_heads, seq // LANES, LANES)
    return pl.pallas_call(
        functools.partial(_att_kernel, tq=tq, heads=heads, cw=cw),
        grid=(batch, hb),
        in_specs=[blk(0), blk(hb), blk(2 * hb), blk(3 * hb),
                  pl.BlockSpec((heads, seq // LANES, LANES), lambda b, i: (b * hb + i, 0, 0))],
        out_specs=pl.BlockSpec((seq, width), lambda b, i: (b, i)),
        out_shape=jax.ShapeDtypeStruct((m, n_heads * HEAD_DIM), BF16),
        scratch_shapes=[pltpu.VMEM((heads * (seq // tq), HEAD_DIM, tq), BF16),
                        pltpu.VMEM((heads, seq, LANES), F32),
                        pltpu.VMEM((n_chains, 1, cw), F32),
                        pltpu.VMEM((n_chains, 1, cw), F32),
                        pltpu.VMEM((n_chains, HEAD_DIM, cw), F32)]
        + [pltpu.VMEM((2, tq, cw), F32)] * n_chains,
        compiler_params=pltpu.CompilerParams(
            dimension_semantics=("parallel", "parallel"),
            vmem_limit_bytes=VMEM_LIMIT),
        name="fox_attention",
    )(proj, proj, proj, proj, cum3)


def _mixers_tile(cb_ref, cc_ref, ch_ref, gc_ref, pu_ref, gp_ref,
                 cch_ref, chh_ref, puh_ref, cw_ref, pw_ref, ps_ref, t_idx):
    ts = cb_ref.shape[0]
    keep = (t_idx > 0).astype(F32)

    def shifted(ext, k):
        return pltpu.roll(ext, k, axis=0)[HALO:]

    u = cc_ref[...].astype(F32) * ch_ref[...].astype(F32)
    u_halo = cch_ref[...].astype(F32) * chh_ref[...].astype(F32) * keep
    ext = jnp.concatenate([u_halo, u], axis=0)
    cw = cw_ref[...]
    y = cw[0:1] * shifted(ext, 2) + cw[1:2] * shifted(ext, 1) + cw[2:3] * u
    y = cb_ref[...].astype(F32) * y
    y_conv = (y * _silu(gc_ref[...].astype(F32))).astype(BF16)

    p = pu_ref[...].astype(F32)
    pext = jnp.concatenate([puh_ref[...].astype(F32) * keep, p], axis=0)
    pos = t_idx * ts + lax.broadcasted_iota(jnp.int32, (ts, 1), 0)
    pg = pw_ref.shape[1]
    outs = []
    for g, w in enumerate(POOL_WINDOWS):
        acc = pext[:, g * pg:(g + 1) * pg]
        k = 1
        while k < w:
            acc = acc + pltpu.roll(acc, k, axis=0)
            k *= 2
        count = jnp.minimum(pos + 1, w).astype(F32)
        z = acc[HALO:] / count - p[:, g * pg:(g + 1) * pg]
        outs.append(jnp.dot(z.astype(BF16), pw_ref[g], preferred_element_type=F32))
    yp = jnp.concatenate(outs, axis=-1) * ps_ref[...]
    y_pool = (yp * _silu(gp_ref[...].astype(F32))).astype(BF16)
    return y_conv, y_pool


def _mix_outproj_kernel(ya_ref, cb_ref, cc_ref, ch_ref, gc_ref, pu_ref, gp_ref,
                        cch_ref, chh_ref, puh_ref, x_ref,
                        cw_ref, pw_ref, ps_ref, wa_ref, wc_ref, wp_ref, g_ref, b_ref,
                        o_ref, ob_ref, *, alpha, rc, tiles_per_seq):
    y0 = jnp.dot(ya_ref[0:rc, :], wa_ref[...], preferred_element_type=F32)
    y_conv, y_pool = _mixers_tile(cb_ref, cc_ref, ch_ref, gc_ref, pu_ref, gp_ref,
                                  cch_ref, chh_ref, puh_ref, cw_ref, pw_ref, ps_ref,
                                  pl.program_id(0) % tiles_per_seq)
    for c in range(x_ref.shape[0] // rc):
        rows = slice(c * rc, (c + 1) * rc)
        y = y0 if c == 0 else jnp.dot(ya_ref[rows, :], wa_ref[...], preferred_element_type=F32)
        y = y + jnp.dot(y_conv[rows, :], wc_ref[...], preferred_element_type=F32)
        y = y + jnp.dot(y_pool[rows, :], wp_ref[...], preferred_element_type=F32)
        r = alpha * x_ref[rows, :] + y
        mu = jnp.mean(r, axis=-1, keepdims=True)
        d = r - mu
        var = jnp.mean(d * d, axis=-1, keepdims=True)
        out = d * lax.rsqrt(var + LN_EPS) * g_ref[...] + b_ref[...]
        o_ref[rows, :] = out
        ob_ref[rows, :] = out.astype(ob_ref.dtype)


def _mix_outproj(y_att, proj, x, conv_w, pool_w, pool_scale, wa, wc, wp, ln_g, ln_b, *,
                 seq, col0, width, alpha, tm, rc):
    m, d = x.shape
    c0 = col0 // width
    rows = lambda a: pl.BlockSpec((tm, a.shape[1]), lambda i: (i, 0))
    cur = lambda j: pl.BlockSpec((tm, width), lambda i: (i, c0 + j))
    hpt = tm // HALO
    halo = lambda j: pl.BlockSpec(
        (HALO, width), lambda i: (jnp.maximum(i * hpt - 1, 0), c0 + j))
    full = lambda a: pl.BlockSpec(a.shape, lambda i: (0,) * a.ndim)
    ps, g2, b2 = pool_scale.reshape(1, width), ln_g.reshape(1, d), ln_b.reshape(1, d)
    consts = (conv_w, pool_w, ps, wa, wc, wp, g2, b2)
    return pl.pallas_call(
        functools.partial(_mix_outproj_kernel, alpha=alpha, rc=rc, tiles_per_seq=seq // tm),
        grid=(m // tm,),
        in_specs=[rows(y_att), cur(0), cur(1), cur(2), cur(3), cur(4), cur(5),
                  halo(1), halo(2), halo(4), rows(x)] + [full(a) for a in consts],
        out_specs=[pl.BlockSpec((tm, d), lambda i: (i, 0))] * 2,
        out_shape=[jax.ShapeDtypeStruct((m, d), F32), jax.ShapeDtypeStruct((m, d), BF16)],
        compiler_params=pltpu.CompilerParams(
            dimension_semantics=("parallel",), vmem_limit_bytes=VMEM_LIMIT),
        name="mix_outproj_deepnorm",
    )(y_att, *([proj] * 9), x, *consts)


def kernel(x, w_in, b_f, conv_w, pool_w, pool_scale, w_out, ln_g, ln_b):
    batch, seq, d = x.shape
    depth = w_in.shape[0]
    n_heads = b_f.shape[1]
    att_w = n_heads * HEAD_DIM
    conv_width = conv_w.shape[2]
    pool_width = pool_scale.shape[1]
    assert conv_width == pool_width
    fg0 = 4 * att_w
    n_main = w_in.shape[2] - n_heads
    alpha = (2 * depth) ** 0.25
    m = batch * seq
    col_scale = jnp.where(jnp.arange(n_main) < att_w, HEAD_DIM ** -0.5 * LOG2E, 1.0)
    col_scale = col_scale.astype(F32).reshape(1, n_main)

    xf = x.reshape(m, d)
    xb = xf
    for l in range(depth):
        w = w_in[l]
        w_main = jnp.concatenate([w[:, :fg0], w[:, fg0 + n_heads:]], axis=1).astype(BF16)
        w_fg = jnp.pad(w[:, fg0:fg0 + n_heads], ((0, 0), (0, LANES - n_heads))).astype(BF16)
        proj, fg = _inproj(xb, w_main, col_scale, w_fg, tm=1024, tn=1024)

        fg_t = fg[:, :n_heads].reshape(batch, seq, n_heads).transpose(0, 2, 1)
        cum = _cum_log_forget(fg_t, b_f[l])
        y_att = _attention(proj, cum, batch=batch, seq=seq, n_heads=n_heads, tq=512, heads=2, cw=256)

        wo = w_out[l].astype(BF16)
        xf, xb = _mix_outproj(y_att, proj, xf, conv_w[l], pool_w[l].astype(BF16), pool_scale[l],
                              wo[:att_w], wo[att_w:att_w + conv_width], wo[att_w + conv_width:],
                              ln_g[l], ln_b[l], seq=seq, col0=fg0, width=conv_width,
                              alpha=alpha, tm=512, rc=256)
    return xf.reshape(batch, seq, d)
```

```python
import functools
import math

import jax
import jax.numpy as jnp
from jax import lax
from jax.experimental import pallas as pl
from jax.experimental.pallas import tpu as pltpu

F32 = jnp.float32
BF16 = jnp.bfloat16

HEAD_DIM = 128
CONV_TAPS = 3
POOL_WINDOWS = (2, 4, 8, 16)
LN_EPS = 1e-5
LANES = 128
HALO = 16
LOG2E = math.log2(math.e)

VMEM_LIMIT = 56 * 1024 * 1024


def _silu(x):
    return x / (1.0 + jnp.exp(-x))


def _inproj_kernel(x_ref, w_ref, cs_ref, wfg_ref, o_ref, fg_ref, *xb_scr):
    xb_ref = xb_scr[0] if xb_scr else x_ref

    @pl.when(pl.program_id(1) == 0)
    def _():
        if xb_scr:
            xb_ref[...] = x_ref[...].astype(BF16)
        fg_ref[...] = jnp.dot(xb_ref[...], wfg_ref[...], preferred_element_type=F32)

    acc = jnp.dot(xb_ref[...], w_ref[...], preferred_element_type=F32)
    o_ref[...] = (acc * cs_ref[...]).astype(o_ref.dtype)


def _inproj(x, w, col_scale, wfg, *, tm, tn):
    m, d = x.shape
    n = w.shape[1]
    scratch = [] if x.dtype == BF16 else [pltpu.VMEM((tm, d), BF16)]
    return pl.pallas_call(
        _inproj_kernel,
        grid=(m // tm, n // tn),
        in_specs=[
            pl.BlockSpec((tm, d), lambda i, j: (i, 0)),
            pl.BlockSpec((d, tn), lambda i, j: (0, j)),
            pl.BlockSpec((1, tn), lambda i, j: (0, j)),
            pl.BlockSpec((d, LANES), lambda i, j: (0, 0)),
        ],
        out_specs=[
            pl.BlockSpec((tm, tn), lambda i, j: (i, j)),
            pl.BlockSpec((tm, LANES), lambda i, j: (i, 0)),
        ],
        out_shape=[
            jax.ShapeDtypeStruct((m, n), BF16),
            jax.ShapeDtypeStruct((m, LANES), F32),
        ],
        scratch_shapes=scratch,
        compiler_params=pltpu.CompilerParams(
            dimension_semantics=("parallel", "arbitrary"),
            vmem_limit_bytes=VMEM_LIMIT),
        name="inproj",
    )(x, w, col_scale, wfg)


def _cum_kernel(fg_ref, bf_ref, o_ref):
    z = fg_ref[0] + bf_ref[...]
    lf = jnp.minimum(z, 0.0) - jnp.log1p(jnp.exp(-jnp.abs(z)))
    s = lf.shape[1]
    lane = lax.broadcasted_iota(jnp.int32, lf.shape, 1)
    k = 1
    while k < s:
        lf = lf + jnp.where(lane >= k, pltpu.roll(lf, k, axis=1), 0.0)
        k *= 2
    o_ref[0] = lf


def _cum_log_forget(fg_t, b_f):
    b, h, s = fg_t.shape
    return pl.pallas_call(
        _cum_kernel,
        grid=(b,),
        in_specs=[
            pl.BlockSpec((1, h, s), lambda i: (i, 0, 0)),
            pl.BlockSpec((h, 1), lambda i: (0, 0)),
        ],
        out_specs=pl.BlockSpec((1, h, s), lambda i: (i, 0, 0)),
        out_shape=jax.ShapeDtypeStruct((b, h, s), F32),
        compiler_params=pltpu.CompilerParams(dimension_semantics=("parallel",)),
        name="cum_log_forget",
    )(fg_t, b_f.reshape(h, 1))


def _att_kernel(q_ref, k_ref, v_ref, g_ref, c_ref, o_ref,
                vt_ref, ckb_ref, m_scr, l_scr, acc_scr, *s_scr, tq, heads, cw):
    s_len = q_ref.shape[0]
    nq = s_len // tq
    groups = s_len // LANES
    krow = lax.broadcasted_iota(jnp.int32, (tq, tq), 0)
    qcol = lax.broadcasted_iota(jnp.int32, (tq, tq), 1)
    causal = krow <= qcol

    for h in range(heads):
        lo = h * HEAD_DIM

        def transpose_v(i, _, h=h, lo=lo):
            rs = pl.multiple_of(i * tq, tq)
            vt_ref[h * nq + i] = v_ref[pl.ds(rs, tq), lo:lo + HEAD_DIM].astype(F32).T.astype(BF16)
            return 0

        lax.fori_loop(0, nq, transpose_v, 0)
        c = c_ref[h] * (-LOG2E)
        ct = jnp.concatenate([c, jnp.zeros((LANES - groups, LANES), F32)], axis=0).T
        for j in range(groups):
            ckb_ref[h, j * LANES:(j + 1) * LANES, :] = jnp.broadcast_to(ct[:, j:j + 1], (LANES, LANES))

    chains = [(h, j) for h in range(heads) for j in range(tq // cw)]
    n = len(chains)

    def scores(c, qi, kj, diag):
        h, j = chains[c]
        cols = slice(h * HEAD_DIM, (h + 1) * HEAD_DIM)
        kl = (j + 1) * cw if diag else tq
        qs = pl.multiple_of(qi * tq + j * cw, cw)
        ks = pl.multiple_of(kj * tq, tq)
        s = lax.dot_general(k_ref[pl.ds(ks, kl), cols], q_ref[pl.ds(qs, cw), cols],
                            (((1,), (1,)), ((), ())), preferred_element_type=F32)
        s = s + jnp.concatenate([ckb_ref[h, pl.ds(ks, kl), :]] * (cw // LANES), axis=1)
        if diag:
            s = jnp.where(causal[:kl, j * cw:(j + 1) * cw], s, -jnp.inf)
        return s

    def softmax(c, slot, kl):
        m = m_scr[c]
        m_new = jnp.maximum(m, jnp.max(s_scr[c][slot, 0:kl, :], axis=0, keepdims=True))
        alpha = jnp.exp2(m - m_new)
        p = jnp.exp2(s_scr[c][slot, 0:kl, :] - m_new)
        l_scr[c] = alpha * l_scr[c] + jnp.sum(p, axis=0, keepdims=True)
        m_scr[c] = m_new
        return alpha, p.astype(BF16)

    def values(c, kj, alpha, p):
        h, _ = chains[c]
        kl = p.shape[0]
        acc_scr[c] = alpha * acc_scr[c] + jnp.dot(vt_ref[h * nq + kj][:, 0:kl], p,
                                                  preferred_element_type=F32)

    def tile_step(qi, kj, slot, nxt, diag=False):
        for c in range(n):
            alpha, p = softmax(c, slot, (chains[c][1] + 1) * cw if diag else tq)
            s_next = scores(c, *nxt)
            values(c, kj, alpha, p)
            s_scr[c][1 - slot, 0:s_next.shape[0], :] = s_next
        return 1 - slot

    for c in range(n):
        s0 = scores(c, 0, 0, True)
        s_scr[c][0, 0:s0.shape[0], :] = s0

    def q_block(qi, slot):
        for c in range(n):
            m_scr[c] = jnp.full((1, cw), -jnp.inf, F32)
            l_scr[c] = jnp.zeros((1, cw), F32)
            acc_scr[c] = jnp.zeros((HEAD_DIM, cw), F32)

        slot = lax.fori_loop(
            0, qi - 1, lambda kj, sl: tile_step(qi, kj, sl, (qi, kj + 1, False)), slot)
        slot = lax.cond(qi > 0, lambda sl: tile_step(qi, qi - 1, sl, (qi, qi, True)),
                        lambda sl: sl, slot)
        slot = tile_step(qi, qi, slot, (jnp.minimum(qi + 1, nq - 1), 0, False), diag=True)
        for c, (h, j) in enumerate(chains):
            rows = pl.ds(pl.multiple_of(qi * tq + j * cw, cw), cw)
            cols = slice(h * HEAD_DIM, (h + 1) * HEAD_DIM)
            gate = _silu(g_ref[rows, cols].astype(F32))
            o_ref[rows, cols] = ((acc_scr[c] / l_scr[c]).T * gate).astype(o_ref.dtype)
        return slot

    lax.fori_loop(0, nq, q_block, 0)


def _attention(proj, cum, *, batch, seq, n_heads, tq, heads, cw):
    m = proj.shape[0]
    hb = n_heads // heads
    width = heads * HEAD_DIM
    n_chains = heads * (tq // cw)
    blk = lambda off: pl.BlockSpec((seq, width), lambda b, i: (b, off + i))
    cum3 = cum.reshape(batch * n_heads, seq // LANES, LANES)
    return pl.pallas_call(
        functools.partial(_att_kernel, tq=tq, heads=heads, cw=cw),
        grid=(batch, hb),
        in_specs=[blk(0), blk(hb), blk(2 * hb), blk(3 * hb),
                  pl.BlockSpec((heads, seq // LANES, LANES), lambda b, i: (b * hb + i, 0, 0))],
        out_specs=pl.BlockSpec((seq, width), lambda b, i: (b, i)),
        out_shape=jax.ShapeDtypeStruct((m, n_heads * HEAD_DIM), BF16),
        scratch_shapes=[pltpu.VMEM((heads * (seq // tq), HEAD_DIM, tq), BF16),
                        pltpu.VMEM((heads, seq, LANES), F32),
                        pltpu.VMEM((n_chains, 1, cw), F32),
                        pltpu.VMEM((n_chains, 1, cw), F32),
                        pltpu.VMEM((n_chains, HEAD_DIM, cw), F32)]
        + [pltpu.VMEM((2, tq, cw), F32)] * n_chains,
        compiler_params=pltpu.CompilerParams(
            dimension_semantics=("parallel", "parallel"),
            vmem_limit_bytes=VMEM_LIMIT),
        name="fox_attention",
    )(proj, proj, proj, proj, cum3)


def _mixers_tile(cb_ref, cc_ref, ch_ref, gc_ref, pu_ref, gp_ref,
                 cch_ref, chh_ref, puh_ref, cw_ref, pw_ref, ps_ref, t_idx):
    ts = cb_ref.shape[0]
    keep = (t_idx > 0).astype(F32)

    def shifted(ext, k):
        return pltpu.roll(ext, k, axis=0)[HALO:]

    u = cc_ref[...].astype(F32) * ch_ref[...].astype(F32)
    u_halo = cch_ref[...].astype(F32) * chh_ref[...].astype(F32) * keep
    ext = jnp.concatenate([u_halo, u], axis=0)
    cw = cw_ref[...]
    y = cw[0:1] * shifted(ext, 2) + cw[1:2] * shifted(ext, 1) + cw[2:3] * u
    y = cb_ref[...].astype(F32) * y
    y_conv = (y * _silu(gc_ref[...].astype(F32))).astype(BF16)

    p = pu_ref[...].astype(F32)
    pext = jnp.concatenate([puh_ref[...].astype(F32) * keep, p], axis=0)
    pos = t_idx * ts + lax.broadcasted_iota(jnp.int32, (ts, 1), 0)
    pg = pw_ref.shape[1]
    outs = []
    for g, w in enumerate(POOL_WINDOWS):
        acc = pext[:, g * pg:(g + 1) * pg]
        k = 1
        while k < w:
            acc = acc + pltpu.roll(acc, k, axis=0)
            k *= 2
        count = jnp.minimum(pos + 1, w).astype(F32)
        z = acc[HALO:] / count - p[:, g * pg:(g + 1) * pg]
        outs.append(jnp.dot(z.astype(BF16), pw_ref[g], preferred_element_type=F32))
    yp = jnp.concatenate(outs, axis=-1) * ps_ref[...]
    y_pool = (yp * _silu(gp_ref[...].astype(F32))).astype(BF16)
    return y_conv, y_pool


def _mix_outproj_kernel(ya_ref, cb_ref, cc_ref, ch_ref, gc_ref, pu_ref, gp_ref,
                        cch_ref, chh_ref, puh_ref, x_ref,
                        cw_ref, pw_ref, ps_ref, wa_ref, wc_ref, wp_ref, g_ref, b_ref,
                        o_ref, ob_ref, *, alpha, rc, tiles_per_seq):
    y0 = jnp.dot(ya_ref[0:rc, :], wa_ref[...], preferred_element_type=F32)
    y_conv, y_pool = _mixers_tile(cb_ref, cc_ref, ch_ref, gc_ref, pu_ref, gp_ref,
                                  cch_ref, chh_ref, puh_ref, cw_ref, pw_ref, ps_ref,
                                  pl.program_id(0) % tiles_per_seq)
    for c in range(x_ref.shape[0] // rc):
        rows = slice(c * rc, (c + 1) * rc)
        y = y0 if c == 0 else jnp.dot(ya_ref[rows, :], wa_ref[...], preferred_element_type=F32)
        y = y + jnp.dot(y_conv[rows, :], wc_ref[...], preferred_element_type=F32)
        y = y + jnp.dot(y_pool[rows, :], wp_ref[...], preferred_element_type=F32)
        r = alpha * x_ref[rows, :] + y
        mu = jnp.mean(r, axis=-1, keepdims=True)
        d = r - mu
        var = jnp.mean(d * d, axis=-1, keepdims=True)
        out = d * lax.rsqrt(var + LN_EPS) * g_ref[...] + b_ref[...]
        o_ref[rows, :] = out
        ob_ref[rows, :] = out.astype(ob_ref.dtype)


def _mix_outproj(y_att, proj, x, conv_w, pool_w, pool_scale, wa, wc, wp, ln_g, ln_b, *,
                 seq, col0, width, alpha, tm, rc):
    m, d = x.shape
    c0 = col0 // width
    rows = lambda a: pl.BlockSpec((tm, a.shape[1]), lambda i: (i, 0))
    cur = lambda j: pl.BlockSpec((tm, width), lambda i: (i, c0 + j))
    hpt = tm // HALO
    halo = lambda j: pl.BlockSpec(
        (HALO, width), lambda i: (jnp.maximum(i * hpt - 1, 0), c0 + j))
    full = lambda a: pl.BlockSpec(a.shape, lambda i: (0,) * a.ndim)
    ps, g2, b2 = pool_scale.reshape(1, width), ln_g.reshape(1, d), ln_b.reshape(1, d)
    consts = (conv_w, pool_w, ps, wa, wc, wp, g2, b2)
    return pl.pallas_call(
        functools.partial(_mix_outproj_kernel, alpha=alpha, rc=rc, tiles_per_seq=seq // tm),
        grid=(m // tm,),
        in_specs=[rows(y_att), cur(0), cur(1), cur(2), cur(3), cur(4), cur(5),
                  halo(1), halo(2), halo(4), rows(x)] + [full(a) for a in consts],
        out_specs=[pl.BlockSpec((tm, d), lambda i: (i, 0))] * 2,
        out_shape=[jax.ShapeDtypeStruct((m, d), F32), jax.ShapeDtypeStruct((m, d), BF16)],
        compiler_params=pltpu.CompilerParams(
            dimension_semantics=("parallel",), vmem_limit_bytes=VMEM_LIMIT),
        name="mix_outproj_deepnorm",
    )(y_att, *([proj] * 9), x, *consts)


def kernel(x, w_in, b_f, conv_w, pool_w, pool_scale, w_out, ln_g, ln_b):
    batch, seq, d = x.shape
    depth = w_in.shape[0]
    n_heads = b_f.shape[1]
    att_w = n_heads * HEAD_DIM
    conv_width = conv_w.shape[2]
    pool_width = pool_scale.shape[1]
    assert conv_width == pool_width
    fg0 = 4 * att_w
    n_main = w_in.shape[2] - n_heads
    alpha = (2 * depth) ** 0.25
    m = batch * seq
    col_scale = jnp.where(jnp.arange(n_main) < att_w, HEAD_DIM ** -0.5 * LOG2E, 1.0)
    col_scale = col_scale.astype(F32).reshape(1, n_main)

    xf = x.reshape(m, d)
    xb = xf
    for l in range(depth):
        w = w_in[l]
        w_main = jnp.concatenate([w[:, :fg0], w[:, fg0 + n_heads:]], axis=1).astype(BF16)
        w_fg = jnp.pad(w[:, fg0:fg0 + n_heads], ((0, 0), (0, LANES - n_heads))).astype(BF16)
        proj, fg = _inproj(xb, w_main, col_scale, w_fg, tm=1024, tn=1024)

        fg_t = fg[:, :n_heads].reshape(batch, seq, n_heads).transpose(0, 2, 1)
        cum = _cum_log_forget(fg_t, b_f[l])
        y_att = _attention(proj, cum, batch=batch, seq=seq, n_heads=n_heads, tq=512, heads=2, cw=256)

        wo = w_out[l].astype(BF16)
        xf, xb = _mix_outproj(y_att, proj, xf, conv_w[l], pool_w[l].astype(BF16), pool_scale[l],
                              wo[:att_w], wo[att_w:att_w + conv_width], wo[att_w + conv_width:],
                              ln_g[l], ln_b[l], seq=seq, col0=fg0, width=conv_width,
                              alpha=alpha, tm=512, rc=256)
    return xf.reshape(batch, seq, d)
```

```python
import functools
import math

import jax
import jax.numpy as jnp
from jax import lax
from jax.experimental import pallas as pl
from jax.experimental.pallas import tpu as pltpu

F32 = jnp.float32
BF16 = jnp.bfloat16

HEAD_DIM = 128
CONV_TAPS = 3
POOL_WINDOWS = (2, 4, 8, 16)
LN_EPS = 1e-5
LANES = 128
HALO = 16
LOG2E = math.log2(math.e)

VMEM_LIMIT = 56 * 1024 * 1024


def _silu(x):
    return x / (1.0 + jnp.exp(-x))


def _inproj_kernel(x_ref, w_ref, cs_ref, wfg_ref, o_ref, fg_ref, *xb_scr):
    xb_ref = xb_scr[0] if xb_scr else x_ref

    @pl.when(pl.program_id(1) == 0)
    def _():
        if xb_scr:
            xb_ref[...] = x_ref[...].astype(BF16)
        fg_ref[...] = jnp.dot(xb_ref[...], wfg_ref[...], preferred_element_type=F32)

    acc = jnp.dot(xb_ref[...], w_ref[...], preferred_element_type=F32)
    o_ref[...] = (acc * cs_ref[...]).astype(o_ref.dtype)


def _inproj(x, w, col_scale, wfg, *, tm, tn):
    m, d = x.shape
    n = w.shape[1]
    scratch = [] if x.dtype == BF16 else [pltpu.VMEM((tm, d), BF16)]
    return pl.pallas_call(
        _inproj_kernel,
        grid=(m // tm, n // tn),
        in_specs=[
            pl.BlockSpec((tm, d), lambda i, j: (i, 0)),
            pl.BlockSpec((d, tn), lambda i, j: (0, j)),
            pl.BlockSpec((1, tn), lambda i, j: (0, j)),
            pl.BlockSpec((d, LANES), lambda i, j: (0, 0)),
        ],
        out_specs=[
            pl.BlockSpec((tm, tn), lambda i, j: (i, j)),
            pl.BlockSpec((tm, LANES), lambda i, j: (i, 0)),
        ],
        out_shape=[
            jax.ShapeDtypeStruct((m, n), BF16),
            jax.ShapeDtypeStruct((m, LANES), F32),
        ],
        scratch_shapes=scratch,
        compiler_params=pltpu.CompilerParams(
            dimension_semantics=("parallel", "arbitrary"),
            vmem_limit_bytes=VMEM_LIMIT),
        name="inproj",
    )(x, w, col_scale, wfg)


def _cum_kernel(fg_ref, bf_ref, o_ref):
    z = fg_ref[0] + bf_ref[...]
    lf = jnp.minimum(z, 0.0) - jnp.log1p(jnp.exp(-jnp.abs(z)))
    s = lf.shape[1]
    lane = lax.broadcasted_iota(jnp.int32, lf.shape, 1)
    k = 1
    while k < s:
        lf = lf + jnp.where(lane >= k, pltpu.roll(lf, k, axis=1), 0.0)
        k *= 2
    o_ref[0] = lf


def _cum_log_forget(fg_t, b_f):
    b, h, s = fg_t.shape
    return pl.pallas_call(
        _cum_kernel,
        grid=(b,),
        in_specs=[
            pl.BlockSpec((1, h, s), lambda i: (i, 0, 0)),
            pl.BlockSpec((h, 1), lambda i: (0, 0)),
        ],
        out_specs=pl.BlockSpec((1, h, s), lambda i: (i, 0, 0)),
        out_shape=jax.ShapeDtypeStruct((b, h, s), F32),
        compiler_params=pltpu.CompilerParams(dimension_semantics=("parallel",)),
        name="cum_log_forget",
    )(fg_t, b_f.reshape(h, 1))


def _att_kernel(q_ref, k_ref, v_ref, g_ref, c_ref, o_ref,
                vt_ref, ckb_ref, m_scr, l_scr, acc_scr, *s_scr, tq, heads, cw):
    s_len = q_ref.shape[0]
    nq = s_len // tq
    groups = s_len // LANES
    krow = lax.broadcasted_iota(jnp.int32, (tq, tq), 0)
    qcol = lax.broadcasted_iota(jnp.int32, (tq, tq), 1)
    causal = krow <= qcol

    for h in range(heads):
        lo = h * HEAD_DIM

        def transpose_v(i, _, h=h, lo=lo):
            rs = pl.multiple_of(i * tq, tq)
            vt_ref[h * nq + i] = v_ref[pl.ds(rs, tq), lo:lo + HEAD_DIM].astype(F32).T.astype(BF16)
            return 0

        lax.fori_loop(0, nq, transpose_v, 0)
        c = c_ref[h] * (-LOG2E)
        ct = jnp.concatenate([c, jnp.zeros((LANES - groups, LANES), F32)], axis=0).T
        for j in range(groups):
            ckb_ref[h, j * LANES:(j + 1) * LANES, :] = jnp.broadcast_to(ct[:, j:j + 1], (LANES, LANES))

    chains = [(h, j) for h in range(heads) for j in range(tq // cw)]
    n = len(chains)

    def scores(c, qi, kj, diag):
        h, j = chains[c]
        cols = slice(h * HEAD_DIM, (h + 1) * HEAD_DIM)
        kl = (j + 1) * cw if diag else tq
        qs = pl.multiple_of(qi * tq + j * cw, cw)
        ks = pl.multiple_of(kj * tq, tq)
        s = lax.dot_general(k_ref[pl.ds(ks, kl), cols], q_ref[pl.ds(qs, cw), cols],
                            (((1,), (1,)), ((), ())), preferred_element_type=F32)
        s = s + jnp.concatenate([ckb_ref[h, pl.ds(ks, kl), :]] * (cw // LANES), axis=1)
        if diag:
            s = jnp.where(causal[:kl, j * cw:(j + 1) * cw], s, -jnp.inf)
        return s

    def softmax(c, slot, kl):
        m = m_scr[c]
        m_new = jnp.maximum(m, jnp.max(s_scr[c][slot, 0:kl, :], axis=0, keepdims=True))
        alpha = jnp.exp2(m - m_new)
        p = jnp.exp2(s_scr[c][slot, 0:kl, :] - m_new)
        l_scr[c] = alpha * l_scr[c] + jnp.sum(p, axis=0, keepdims=True)
        m_scr[c] = m_new
        return alpha, p.astype(BF16)

    def values(c, kj, alpha, p):
        h, _ = chains[c]
        kl = p.shape[0]
        acc_scr[c] = alpha * acc_scr[c] + jnp.dot(vt_ref[h * nq + kj][:, 0:kl], p,
                                                  preferred_element_type=F32)

    def tile_step(qi, kj, slot, nxt, diag=False):
        for c in range(n):
            alpha, p = softmax(c, slot, (chains[c][1] + 1) * cw if diag else tq)
            s_next = scores(c, *nxt)
            values(c, kj, alpha, p)
            s_scr[c][1 - slot, 0:s_next.shape[0], :] = s_next
        return 1 - slot

    for c in range(n):
        s0 = scores(c, 0, 0, True)
        s_scr[c][0, 0:s0.shape[0], :] = s0

    def q_block(qi, slot):
        for c in range(n):
            m_scr[c] = jnp.full((1, cw), -jnp.inf, F32)
            l_scr[c] = jnp.zeros((1, cw), F32)
            acc_scr[c] = jnp.zeros((HEAD_DIM, cw), F32)

        n_s = jnp.maximum(qi - 1, 0)

        def pair(i, sl):
            sl = tile_step(qi, 2 * i, sl, (qi, 2 * i + 1, False))
            return tile_step(qi, 2 * i + 1, sl, (qi, 2 * i + 2, False))

        slot = lax.fori_loop(0, n_s // 2, pair, slot)
        slot = lax.cond(n_s % 2 == 1, lambda sl: tile_step(qi, n_s - 1, sl, (qi, n_s, False)),
                        lambda sl: sl, slot)
        slot = lax.cond(qi > 0, lambda sl: tile_step(qi, qi - 1, sl, (qi, qi, True)),
                        lambda sl: sl, slot)
        slot = tile_step(qi, qi, slot, (jnp.minimum(qi + 1, nq - 1), 0, False), diag=True)
        for c, (h, j) in enumerate(chains):
            rows = pl.ds(pl.multiple_of(qi * tq + j * cw, cw), cw)
            cols = slice(h * HEAD_DIM, (h + 1) * HEAD_DIM)
            gate = _silu(g_ref[rows, cols].astype(F32))
            o_ref[rows, cols] = ((acc_scr[c] / l_scr[c]).T * gate).astype(o_ref.dtype)
        return slot

    lax.fori_loop(0, nq, q_block, 0)


def _attention(proj, cum, *, batch, seq, n_heads, tq, heads, cw):
    m = proj.shape[0]
    hb = n_heads // heads
    width = heads * HEAD_DIM
    n_chains = heads * (tq // cw)
    blk = lambda off: pl.BlockSpec((seq, width), lambda b, i: (b, off + i))
    cum3 = cum.reshape(batch * n_heads, seq // LANES, LANES)
    return pl.pallas_call(
        functools.partial(_att_kernel, tq=tq, heads=heads, cw=cw),
        grid=(batch, hb),
        in_specs=[blk(0), blk(hb), blk(2 * hb), blk(3 * hb),
                  pl.BlockSpec((heads, seq // LANES, LANES), lambda b, i: (b * hb + i, 0, 0))],
        out_specs=pl.BlockSpec((seq, width), lambda b, i: (b, i)),
        out_shape=jax.ShapeDtypeStruct((m, n_heads * HEAD_DIM), BF16),
        scratch_shapes=[pltpu.VMEM((heads * (seq // tq), HEAD_DIM, tq), BF16),
                        pltpu.VMEM((heads, seq, LANES), F32),
                        pltpu.VMEM((n_chains, 1, cw), F32),
                        pltpu.VMEM((n_chains, 1, cw), F32),
                        pltpu.VMEM((n_chains, HEAD_DIM, cw), F32)]
        + [pltpu.VMEM((2, tq, cw), F32)] * n_chains,
        compiler_params=pltpu.CompilerParams(
            dimension_semantics=("parallel", "parallel"),
            vmem_limit_bytes=VMEM_LIMIT),
        name="fox_attention",
    )(proj, proj, proj, proj, cum3)


def _mixers_tile(cb_ref, cc_ref, ch_ref, gc_ref, pu_ref, gp_ref,
                 cch_ref, chh_ref, puh_ref, cw_ref, pw_ref, ps_ref, t_idx):
    ts = cb_ref.shape[0]
    keep = (t_idx > 0).astype(F32)

    def shifted(ext, k):
        return pltpu.roll(ext, k, axis=0)[HALO:]

    u = cc_ref[...].astype(F32) * ch_ref[...].astype(F32)
    u_halo = cch_ref[...].astype(F32) * chh_ref[...].astype(F32) * keep
    ext = jnp.concatenate([u_halo, u], axis=0)
    cw = cw_ref[...]
    y = cw[0:1] * shifted(ext, 2) + cw[1:2] * shifted(ext, 1) + cw[2:3] * u
    y = cb_ref[...].astype(F32) * y
    y_conv = (y * _silu(gc_ref[...].astype(F32))).astype(BF16)

    p = pu_ref[...].astype(F32)
    pext = jnp.concatenate([puh_ref[...].astype(F32) * keep, p], axis=0)
    pos = t_idx * ts + lax.broadcasted_iota(jnp.int32, (ts, 1), 0)
    pg = pw_ref.shape[1]
    outs = []
    for g, w in enumerate(POOL_WINDOWS):
        acc = pext[:, g * pg:(g + 1) * pg]
        k = 1
        while k < w:
            acc = acc + pltpu.roll(acc, k, axis=0)
            k *= 2
        count = jnp.minimum(pos + 1, w).astype(F32)
        z = acc[HALO:] / count - p[:, g * pg:(g + 1) * pg]
        outs.append(jnp.dot(z.astype(BF16), pw_ref[g], preferred_element_type=F32))
    yp = jnp.concatenate(outs, axis=-1) * ps_ref[...]
    y_pool = (yp * _silu(gp_ref[...].astype(F32))).astype(BF16)
    return y_conv, y_pool


def _mix_outproj_kernel(ya_ref, cb_ref, cc_ref, ch_ref, gc_ref, pu_ref, gp_ref,
                        cch_ref, chh_ref, puh_ref, x_ref,
                        cw_ref, pw_ref, ps_ref, wa_ref, wc_ref, wp_ref, g_ref, b_ref,
                        o_ref, ob_ref, *, alpha, rc, tiles_per_seq):
    y0 = jnp.dot(ya_ref[0:rc, :], wa_ref[...], preferred_element_type=F32)
    y_conv, y_pool = _mixers_tile(cb_ref, cc_ref, ch_ref, gc_ref, pu_ref, gp_ref,
                                  cch_ref, chh_ref, puh_ref, cw_ref, pw_ref, ps_ref,
                                  pl.program_id(0) % tiles_per_seq)
    for c in range(x_ref.shape[0] // rc):
        rows = slice(c * rc, (c + 1) * rc)
        y = y0 if c == 0 else jnp.dot(ya_ref[rows, :], wa_ref[...], preferred_element_type=F32)
        y = y + jnp.dot(y_conv[rows, :], wc_ref[...], preferred_element_type=F32)
        y = y + jnp.dot(y_pool[rows, :], wp_ref[...], preferred_element_type=F32)
        r = alpha * x_ref[rows, :] + y
        mu = jnp.mean(r, axis=-1, keepdims=True)
        d = r - mu
        var = jnp.mean(d * d, axis=-1, keepdims=True)
        out = d * lax.rsqrt(var + LN_EPS) * g_ref[...] + b_ref[...]
        o_ref[rows, :] = out
        ob_ref[rows, :] = out.astype(ob_ref.dtype)


def _mix_outproj(y_att, proj, x, conv_w, pool_w, pool_scale, wa, wc, wp, ln_g, ln_b, *,
                 seq, col0, width, alpha, tm, rc):
    m, d = x.shape
    c0 = col0 // width
    rows = lambda a: pl.BlockSpec((tm, a.shape[1]), lambda i: (i, 0))
    cur = lambda j: pl.BlockSpec((tm, width), lambda i: (i, c0 + j))
    hpt = tm // HALO
    halo = lambda j: pl.BlockSpec(
        (HALO, width), lambda i: (jnp.maximum(i * hpt - 1, 0), c0 + j))
    full = lambda a: pl.BlockSpec(a.shape, lambda i: (0,) * a.ndim)
    ps, g2, b2 = pool_scale.reshape(1, width), ln_g.reshape(1, d), ln_b.reshape(1, d)
    consts = (conv_w, pool_w, ps, wa, wc, wp, g2, b2)
    return pl.pallas_call(
        functools.partial(_mix_outproj_kernel, alpha=alpha, rc=rc, tiles_per_seq=seq // tm),
        grid=(m // tm,),
        in_specs=[rows(y_att), cur(0), cur(1), cur(2), cur(3), cur(4), cur(5),
                  halo(1), halo(2), halo(4), rows(x)] + [full(a) for a in consts],
        out_specs=[pl.BlockSpec((tm, d), lambda i: (i, 0))] * 2,
        out_shape=[jax.ShapeDtypeStruct((m, d), F32), jax.ShapeDtypeStruct((m, d), BF16)],
        compiler_params=pltpu.CompilerParams(
            dimension_semantics=("parallel",), vmem_limit_bytes=VMEM_LIMIT),
        name="mix_outproj_deepnorm",
    )(y_att, *([proj] * 9), x, *consts)


def kernel(x, w_in, b_f, conv_w, pool_w, pool_scale, w_out, ln_g, ln_b):
    batch, seq, d = x.shape
    depth = w_in.shape[0]
    n_heads = b_f.shape[1]
    att_w = n_heads * HEAD_DIM
    conv_width = conv_w.shape[2]
    pool_width = pool_scale.shape[1]
    assert conv_width == pool_width
    fg0 = 4 * att_w
    n_main = w_in.shape[2] - n_heads
    alpha = (2 * depth) ** 0.25
    m = batch * seq
    col_scale = jnp.where(jnp.arange(n_main) < att_w, HEAD_DIM ** -0.5 * LOG2E, 1.0)
    col_scale = col_scale.astype(F32).reshape(1, n_main)

    xf = x.reshape(m, d)
    xb = xf
    for l in range(depth):
        w = w_in[l]
        w_main = jnp.concatenate([w[:, :fg0], w[:, fg0 + n_heads:]], axis=1).astype(BF16)
        w_fg = jnp.pad(w[:, fg0:fg0 + n_heads], ((0, 0), (0, LANES - n_heads))).astype(BF16)
        proj, fg = _inproj(xb, w_main, col_scale, w_fg, tm=1024, tn=1024)

        fg_t = fg[:, :n_heads].reshape(batch, seq, n_heads).transpose(0, 2, 1)
        cum = _cum_log_forget(fg_t, b_f[l])
        y_att = _attention(proj, cum, batch=batch, seq=seq, n_heads=n_heads, tq=512, heads=2, cw=256)

        wo = w_out[l].astype(BF16)
        xf, xb = _mix_outproj(y_att, proj, xf, conv_w[l], pool_w[l].astype(BF16), pool_scale[l],
                              wo[:att_w], wo[att_w:att_w + conv_width], wo[att_w + conv_width:],
                              ln_g[l], ln_b[l], seq=seq, col0=fg0, width=conv_width,
                              alpha=alpha, tm=512, rc=256)
    return xf.reshape(batch, seq, d)
```
